```python
import math
import jax, jax.numpy as jnp
from jax import lax
import numpy as np

D_MODEL = 2048
BATCH = 2
SEQ = 4096
DEPTH = 4
DEC_BATCH = 128
DEC_SEQ = 1
PAST_LEN = 8192
PAGE_SIZE = 128

N_DIFF_HEADS = 8
DIFF_HEAD_DIM = 64
DIFF_QK_DIM = 2 * DIFF_HEAD_DIM
DIFF_V_DIM = 2 * DIFF_HEAD_DIM
N_MLA_HEADS = 8
MLA_NOPE_DIM = 128
ROPE_DIM = 64
MLA_V_DIM = 128
KV_RANK = 256
ROPE_THETA = 10000.0
N_BUCKETS = 32
T5_EXACT = N_BUCKETS // 2
T5_MAX_DIST = 128
D_FF = ((8 * D_MODEL // 3 + 127) // 128) * 128
PLE_DIM = 256
Q_BLOCK = 128
EPS = 1e-6

DIFF_Q_COLS = N_DIFF_HEADS * DIFF_QK_DIM
MLA_Q_COLS = N_MLA_HEADS * (MLA_NOPE_DIM + ROPE_DIM)
IN_COLS = DIFF_Q_COLS + DIFF_QK_DIM + DIFF_V_DIM + MLA_Q_COLS + KV_RANK + ROPE_DIM + 2 * D_MODEL
DIFF_OUT = N_DIFF_HEADS * DIFF_V_DIM
MLA_OUT = N_MLA_HEADS * MLA_V_DIM
DIFF_SCALE = DIFF_HEAD_DIM ** -0.5
MLA_SCALE = (MLA_NOPE_DIM + ROPE_DIM) ** -0.5

kernel_name = "hybrid_diffattn_mla_macaron_decoder_step"


def _rmsnorm(x, g):
    xf = x.astype(jnp.float32)
    y = xf * lax.rsqrt(jnp.mean(xf * xf, axis=-1, keepdims=True) + EPS)
    return y.astype(x.dtype) * g


def _swiglu(x, w_in, w_out):
    g, u = jnp.split(x @ w_in, 2, axis=-1)
    return (jax.nn.silu(g) * u) @ w_out


def _rope(x, pos):
    half = ROPE_DIM // 2
    inv = ROPE_THETA ** (-jnp.arange(half, dtype=jnp.float32) / half)
    ang = pos.astype(jnp.float32)[:, None] * inv[None, :]
    if x.ndim == 4:
        ang = ang[:, None, :]
    cos = jnp.cos(ang).astype(x.dtype)
    sin = jnp.sin(ang).astype(x.dtype)
    x1, x2 = x[..., :half], x[..., half:]
    return jnp.concatenate([x1 * cos - x2 * sin, x2 * cos + x1 * sin], axis=-1)


def _t5_bias(q_pos, k_pos, table):
    n = jnp.maximum(q_pos[:, None] - k_pos[None, :], 0)
    large = T5_EXACT + (jnp.log(jnp.maximum(n, 1).astype(jnp.float32) / T5_EXACT)
                        / math.log(T5_MAX_DIST / T5_EXACT) * (N_BUCKETS - T5_EXACT)).astype(jnp.int32)
    bucket = jnp.where(n < T5_EXACT, n, jnp.minimum(large, N_BUCKETS - 1))
    return jnp.moveaxis(table[bucket], -1, 0).astype(jnp.float32)


def _sweep_queries(fn, qs, q_pos):
    B, Sq = qs[0].shape[:2]
    if Sq > Q_BLOCK and Sq % Q_BLOCK == 0:
        nb = Sq // Q_BLOCK
        qbs = tuple(jnp.moveaxis(a.reshape(B, nb, Q_BLOCK, *a.shape[2:]), 1, 0) for a in qs)
        out = lax.map(lambda args: fn(args[0], args[1]), (qbs, q_pos.reshape(nb, Q_BLOCK)))
        return jnp.moveaxis(out, 0, 1).reshape(B, Sq, *out.shape[3:])
    return fn(qs, q_pos)


def _diff_core(qs, q_pos, segs, lam, rel_bias):
    (q,) = qs
    B, Sq = q.shape[:2]
    q = q.reshape(B, Sq, N_DIFF_HEADS, 2, DIFF_HEAD_DIM)
    logits = []
    for k, _, kp in segs:
        kk = k.reshape(k.shape[0], k.shape[1], 2, DIFF_HEAD_DIM)
        s = jnp.einsum('bqhmd,bkmd->mbhqk', q, kk).astype(jnp.float32) * DIFF_SCALE + _t5_bias(q_pos, kp, rel_bias)
        logits.append(jnp.where(kp[None, :] <= q_pos[:, None], s, -jnp.inf))
    p = jax.nn.softmax(jnp.concatenate(logits, axis=-1), axis=-1)
    a = p[0] - lam * p[1]
    out = None
    start = 0
    for _, v, kp in segs:
        n = kp.shape[0]
        o = jnp.einsum('bhqk,bkd->bqhd', a[..., start:start + n].astype(v.dtype), v)
        out = o if out is None else out + o
        start += n
    return out


def _mla_core(qs, q_pos, segs):
    q_lat, q_rope = qs
    logits = []
    for ckv, kr, kp in segs:
        s = (jnp.einsum('bqhc,bkc->bhqk', q_lat, ckv)
             + jnp.einsum('bqhr,bkr->bhqk', q_rope, kr)).astype(jnp.float32) * MLA_SCALE
        logits.append(jnp.where(kp[None, :] <= q_pos[:, None], s, -jnp.inf))
    p = jax.nn.softmax(jnp.concatenate(logits, axis=-1), axis=-1)
    out = None
    start = 0
    for ckv, _, kp in segs:
        n = kp.shape[0]
        o = jnp.einsum('bhqk,bkc->bqhc', p[..., start:start + n].astype(ckv.dtype), ckv)
        out = o if out is None else out + o
        start += n
    return out


def _layer(x, p_i, start, past, lam_init, rel_bias, lw):
    (n_ffn1, w_ffn1_in, w_ffn1_out, n_mix, w_in, lq1, lk1, lq2, lk2, diff_subln, kv_norm,
     w_uk, w_uv, w_branch_a, w_branch_b, w_out, n_ffn2, w_ffn2_in, w_ffn2_out,
     n_ple, w_ple_gate, w_ple_proj) = lw
    B, S, _ = x.shape
    pos = start + jnp.arange(S, dtype=jnp.int32)
    h = x + 0.5 * _swiglu(_rmsnorm(x, n_ffn1), w_ffn1_in, w_ffn1_out)
    u = _rmsnorm(h, n_mix)
    sizes = (DIFF_Q_COLS, DIFF_QK_DIM, DIFF_V_DIM, MLA_Q_COLS, KV_RANK, ROPE_DIM, D_MODEL, D_MODEL)
    split_at = np.cumsum(sizes)[:-1].tolist()
    dq, dk, dv, mq, c_kv, k_rope, g_a, g_b = jnp.split(u @ w_in, split_at, axis=-1)
    dq = dq.reshape(B, S, N_DIFF_HEADS, DIFF_QK_DIM)
    mq = mq.reshape(B, S, N_MLA_HEADS, MLA_NOPE_DIM + ROPE_DIM)
    q_nope = mq[..., :MLA_NOPE_DIM]
    q_rope = _rope(mq[..., MLA_NOPE_DIM:], pos)
    c_kv = _rmsnorm(c_kv, kv_norm)
    k_rope = _rope(k_rope, pos)
    if past is None:
        segs_d = [(dk, dv, pos)]
        segs_m = [(c_kv, k_rope, pos)]
    else:
        k_past, v_past, ckv_past, kr_past = past
        past_pos = jnp.arange(k_past.shape[1], dtype=jnp.int32)
        segs_d = [(k_past, v_past, past_pos), (dk, dv, pos)]
        segs_m = [(ckv_past, kr_past, past_pos), (c_kv, k_rope, pos)]
    lam = (jnp.exp(jnp.sum(lq1.astype(jnp.float32) * lk1.astype(jnp.float32)))
           - jnp.exp(jnp.sum(lq2.astype(jnp.float32) * lk2.astype(jnp.float32))) + lam_init)
    o_a = _sweep_queries(lambda qs, qp: _diff_core(qs, qp, segs_d, lam, rel_bias), (dq,), pos)
    o_a = (_rmsnorm(o_a, diff_subln) * (1.0 - lam_init)).reshape(B, S, DIFF_OUT)
    q_lat = jnp.einsum('bshd,chd->bshc', q_nope, w_uk)
    o_lat = _sweep_queries(lambda qs, qp: _mla_core(qs, qp, segs_m), (q_lat, q_rope), pos)
    o_b = jnp.einsum('bshc,chd->bshd', o_lat, w_uv).reshape(B, S, MLA_OUT)
    merged = jax.nn.sigmoid(g_a) * (o_a @ w_branch_a) + jax.nn.sigmoid(g_b) * (o_b @ w_branch_b)
    h = h + merged @ w_out
    h = h + 0.5 * _swiglu(_rmsnorm(h, n_ffn2), w_ffn2_in, w_ffn2_out)
    h = h + jax.nn.sigmoid(_rmsnorm(h, n_ple) @ w_ple_gate) * (p_i @ w_ple_proj)
    return h, (dk, dv, c_kv, k_rope)


def setup_inputs(seed: int = 0) -> dict:
    key = jax.random.key(seed)
    ks = jax.random.split(key, 40)
    f32 = jnp.float32

    def dense(k, shape, fan_in):
        return jax.random.normal(k, shape, f32) * fan_in ** -0.5

    def gain(k, shape):
        return 1.0 + 0.1 * jax.random.normal(k, shape, f32)

    n_pages = PAST_LEN // PAGE_SIZE
    used = DEC_BATCH * n_pages
    n_pool = used + max(1, used // 4)
    page_table = jax.random.permutation(ks[0], n_pool)[:used].reshape(DEC_BATCH, n_pages).astype(jnp.int32)
    return {
        "x_prompt": jax.random.normal(ks[1], (BATCH, SEQ, D_MODEL), f32),
        "x_sample": jax.random.normal(ks[2], (DEC_BATCH, DEC_SEQ, D_MODEL), f32),
        "cache_diff_k": jax.random.normal(ks[3], (DEPTH, n_pool, PAGE_SIZE, DIFF_QK_DIM), f32),
        "cache_diff_v": jax.random.normal(ks[4], (DEPTH, n_pool, PAGE_SIZE, DIFF_V_DIM), f32),
        "cache_mla_ckv": jax.random.normal(ks[5], (DEPTH, n_pool, PAGE_SIZE, KV_RANK), f32),
        "cache_mla_krope": jax.random.normal(ks[6], (DEPTH, n_pool, PAGE_SIZE, ROPE_DIM), f32),
        "page_table": page_table,
        "p_prompt": jax.random.normal(ks[7], (DEPTH, BATCH, SEQ, PLE_DIM), f32),
        "p_sample": jax.random.normal(ks[8], (DEPTH, DEC_BATCH, DEC_SEQ, PLE_DIM), f32),
        "rel_bias": 0.5 * jax.random.normal(ks[9], (N_BUCKETS, N_DIFF_HEADS), f32),
        "norm_ffn1": gain(ks[10], (DEPTH, D_MODEL)),
        "w_ffn1_in": dense(ks[11], (DEPTH, D_MODEL, 2 * D_FF), D_MODEL),
        "w_ffn1_out": dense(ks[12], (DEPTH, D_FF, D_MODEL), D_FF),
        "norm_mix": gain(ks[13], (DEPTH, D_MODEL)),
        "w_in": dense(ks[14], (DEPTH, D_MODEL, IN_COLS), D_MODEL),
        "diff_lambda_q1": 0.1 * jax.random.normal(ks[15], (DEPTH, DIFF_HEAD_DIM), f32),
        "diff_lambda_k1": 0.1 * jax.random.normal(ks[16], (DEPTH, DIFF_HEAD_DIM), f32),
        "diff_lambda_q2": 0.1 * jax.random.normal(ks[17], (DEPTH, DIFF_HEAD_DIM), f32),
        "diff_lambda_k2": 0.1 * jax.random.normal(ks[18], (DEPTH, DIFF_HEAD_DIM), f32),
        "diff_subln": gain(ks[19], (DEPTH, DIFF_V_DIM)),
        "mla_kv_norm": gain(ks[20], (DEPTH, KV_RANK)),
        "mla_w_uk": dense(ks[21], (DEPTH, KV_RANK, N_MLA_HEADS, MLA_NOPE_DIM), KV_RANK),
        "mla_w_uv": dense(ks[22], (DEPTH, KV_RANK, N_MLA_HEADS, MLA_V_DIM), KV_RANK),
        "w_branch_a": dense(ks[23], (DEPTH, DIFF_OUT, D_MODEL), DIFF_OUT),
        "w_branch_b": dense(ks[24], (DEPTH, MLA_OUT, D_MODEL), MLA_OUT),
        "w_out": dense(ks[25], (DEPTH, D_MODEL, D_MODEL), D_MODEL),
        "norm_ffn2": gain(ks[26], (DEPTH, D_MODEL)),
        "w_ffn2_in": dense(ks[27], (DEPTH, D_MODEL, 2 * D_FF), D_MODEL),
        "w_ffn2_out": dense(ks[28], (DEPTH, D_FF, D_MODEL), D_FF),
        "norm_ple": gain(ks[29], (DEPTH, D_MODEL)),
        "w_ple_gate": dense(ks[30], (DEPTH, D_MODEL, D_MODEL), D_MODEL),
        "w_ple_proj": dense(ks[31], (DEPTH, PLE_DIM, D_MODEL), PLE_DIM),
        "norm_final": gain(ks[32], (D_MODEL,)),
    }


def reference(x_prompt, x_sample, cache_diff_k, cache_diff_v, cache_mla_ckv, cache_mla_krope, page_table,
              p_prompt, p_sample, rel_bias, norm_ffn1, w_ffn1_in, w_ffn1_out, norm_mix, w_in,
              diff_lambda_q1, diff_lambda_k1, diff_lambda_q2, diff_lambda_k2, diff_subln, mla_kv_norm,
              mla_w_uk, mla_w_uv, w_branch_a, w_branch_b, w_out, norm_ffn2, w_ffn2_in, w_ffn2_out,
              norm_ple, w_ple_gate, w_ple_proj, norm_final):
    dec_b = page_table.shape[0]
    past_len = page_table.shape[1] * cache_diff_k.shape[2]
    hp, hs = x_prompt, x_sample
    rows_p = ([], [], [], [])
    rows_s = ([], [], [], [])
    for i in range(DEPTH):
        lw = (norm_ffn1[i], w_ffn1_in[i], w_ffn1_out[i], norm_mix[i], w_in[i],
              diff_lambda_q1[i], diff_lambda_k1[i], diff_lambda_q2[i], diff_lambda_k2[i], diff_subln[i],
              mla_kv_norm[i], mla_w_uk[i], mla_w_uv[i], w_branch_a[i], w_branch_b[i], w_out[i],
              norm_ffn2[i], w_ffn2_in[i], w_ffn2_out[i], norm_ple[i], w_ple_gate[i], w_ple_proj[i])
        lam_init = 0.8 - 0.6 * math.exp(-0.3 * i)
        hp, new_p = _layer(hp, p_prompt[i], 0, None, lam_init, rel_bias, lw)
        past = (cache_diff_k[i, page_table].reshape(dec_b, past_len, DIFF_QK_DIM),
                cache_diff_v[i, page_table].reshape(dec_b, past_len, DIFF_V_DIM),
                cache_mla_ckv[i, page_table].reshape(dec_b, past_len, KV_RANK),
                cache_mla_krope[i, page_table].reshape(dec_b, past_len, ROPE_DIM))
        hs, new_s = _layer(hs, p_sample[i], past_len, past, lam_init, rel_bias, lw)
        for lst, r in zip(rows_p, new_p):
            lst.append(r)
        for lst, r in zip(rows_s, new_s):
            lst.append(r)
    y_prompt = _rmsnorm(hp, norm_final)
    y_sample = _rmsnorm(hs, norm_final)
    prompt_diff_k = jnp.stack(rows_p[0])
    prompt_diff_v = jnp.stack(rows_p[1])
    prompt_mla_ckv = jnp.stack(rows_p[2])
    prompt_mla_krope = jnp.stack(rows_p[3])
    sample_diff_k = jnp.stack(rows_s[0])
    sample_diff_v = jnp.stack(rows_s[1])
    sample_mla_ckv = jnp.stack(rows_s[2])
    sample_mla_krope = jnp.stack(rows_s[3])
    return (y_prompt, y_sample, prompt_diff_k, prompt_diff_v, prompt_mla_ckv, prompt_mla_krope,
            sample_diff_k, sample_diff_v, sample_mla_ckv, sample_mla_krope)
```

```python
import functools
import math

import jax
import jax.numpy as jnp
from jax import lax
from jax.experimental import pallas as pl
from jax.experimental.pallas import tpu as pltpu

F32 = jnp.float32
BF16 = jnp.bfloat16

D_MODEL = 2048
DEPTH = 4
PAGE = 128
N_HEADS = 8
DIFF_HEAD_DIM = 64
DIFF_QK = 128
DIFF_V = 128
MLA_NOPE = 128
ROPE_DIM = 64
MLA_V = 128
KV_RANK = 256
ROPE_THETA = 10000.0
N_BUCKETS = 32
T5_EXACT = 16
T5_MAX_DIST = 128
D_FF = 5504
PLE_DIM = 256
EPS = 1e-6
DIFF_SCALE = DIFF_HEAD_DIM ** -0.5
MLA_SCALE = (MLA_NOPE + ROPE_DIM) ** -0.5

LANES = 128
D_FF_PAD = 5632
FFN_TF = 512
ATTN_T = 256
QKV_TM = 256
NEG_BIG = -1e30
VMEM_LIMIT = 56 * 1024 * 1024

_OFF_DQ = 0
_OFF_DK = 1024
_OFF_DV = 1152
_OFF_MQ = 1280
_OFF_CKV = 2816
_OFF_KR = 3072
_OFF_GA = 3136
_OFF_GB = 5184


def _t5_thresholds():
    out = []
    for k in range(1, N_BUCKETS - T5_EXACT):
        n = T5_EXACT
        while int(math.log(n / T5_EXACT) / math.log(T5_MAX_DIST / T5_EXACT) * (N_BUCKETS - T5_EXACT)) < k:
            n += 1
        out.append(n)
    return tuple(out)


T5_THRESHOLDS = _t5_thresholds()
T5_FAR = T5_THRESHOLDS[-1]


def _cparams(sem):
    return pltpu.CompilerParams(dimension_semantics=sem, vmem_limit_bytes=VMEM_LIMIT)


def _rms(x, g):
    ms = jnp.mean(x * x, axis=-1, keepdims=True)
    return (x * lax.rsqrt(ms + EPS)) * g


def _dot(a, b):
    return jnp.dot(a, b, preferred_element_type=F32)


def _dot_nt(a, b):
    return lax.dot_general(a, b, (((1,), (1,)), ((), ())), preferred_element_type=F32)


def _ffn_kernel(x_ref, g_ref, wg_ref, wu_ref, wo_ref, o_ref, xn_ref, acc_ref):
    j = pl.program_id(1)

    @pl.when(j == 0)
    def _():
        xn_ref[...] = _rms(x_ref[...], g_ref[...]).astype(BF16)
        acc_ref[...] = jnp.zeros_like(acc_ref)

    xn = xn_ref[...]
    g = _dot(xn, wg_ref[...])
    u = _dot(xn, wu_ref[...])
    a = (g * jax.nn.sigmoid(g) * u).astype(BF16)
    acc_ref[...] += _dot(a, wo_ref[...])

    @pl.when(j == pl.num_programs(1) - 1)
    def _():
        o_ref[...] = x_ref[...] + 0.5 * acc_ref[...]


def _ffn(x, g, wg, wu, wo, tm):
    m = x.shape[0]
    nj = D_FF_PAD // FFN_TF
    return pl.pallas_call(
        _ffn_kernel,
        grid=(m // tm, nj),
        in_specs=[
            pl.BlockSpec((tm, D_MODEL), lambda i, j: (i, 0)),
            pl.BlockSpec((1, D_MODEL), lambda i, j: (0, 0)),
            pl.BlockSpec((D_MODEL, FFN_TF), lambda i, j: (0, j)),
            pl.BlockSpec((D_MODEL, FFN_TF), lambda i, j: (0, j)),
            pl.BlockSpec((FFN_TF, D_MODEL), lambda i, j: (j, 0)),
        ],
        out_specs=pl.BlockSpec((tm, D_MODEL), lambda i, j: (i, 0)),
        out_shape=jax.ShapeDtypeStruct((m, D_MODEL), F32),
        scratch_shapes=[pltpu.VMEM((tm, D_MODEL), BF16), pltpu.VMEM((tm, D_MODEL), F32)],
        compiler_params=_cparams(("parallel", "arbitrary")),
        name="ffn",
    )(x, g, wg, wu, wo)


def _qkv_kernel(h_ref, g_ref, wa_ref, wb_ref, wuk_ref, kvn_ref, cos_ref, sinm_ref, cs_ref,
                dq_ref, qlat_ref, qrope_ref, kb_ref, vb_ref, kcat_ref,
                dk_ref, dv_ref, ckv_ref, kr_ref):
    u = _rms(h_ref[...], g_ref[...]).astype(BF16)
    dq_ref[...] = (_dot(u, wa_ref[:, 0:1024]) * DIFF_SCALE).astype(BF16)
    for hd in range(N_HEADS):
        qn = _dot(u, wa_ref[:, 1024 + hd * MLA_NOPE:1024 + (hd + 1) * MLA_NOPE]).astype(BF16)
        qlat_ref[:, hd * KV_RANK:(hd + 1) * KV_RANK] = _dot(qn, wuk_ref[hd]).astype(BF16)
    cos = cos_ref[...]
    sinm = sinm_ref[...]
    for gi in range(4):
        y = _dot(u, wa_ref[:, 2048 + gi * LANES:2048 + (gi + 1) * LANES])
        ys = _dot(u, wa_ref[:, 2560 + gi * LANES:2560 + (gi + 1) * LANES])
        qrope_ref[:, gi * LANES:(gi + 1) * LANES] = (y * cos + ys * sinm).astype(BF16)
    kv = _dot(u, wb_ref[...])
    dk = kv[:, 0:128]
    dv = kv[:, 128:256]
    dk_ref[...] = dk
    dv_ref[...] = dv
    kb_ref[...] = dk.astype(BF16)
    vb_ref[...] = dv.astype(BF16)
    ckv = _rms(kv[:, 256:512], kvn_ref[...])
    ckv_ref[...] = ckv
    kcat_ref[:, 0:KV_RANK] = ckv.astype(BF16)
    t = kv[:, 512:640] * cs_ref[...]
    kr2 = t + pltpu.roll(t, 64, 1)
    kr_ref[...] = kr2[:, 0:ROPE_DIM]
    kcat_ref[:, KV_RANK:KV_RANK + LANES] = kr2.astype(BF16)


def _qkv(h, g, wa, wb, wuk, kvn, cos, sinm, cs, tm):
    m = h.shape[0]
    row = lambda n: pl.BlockSpec((tm, n), lambda i: (i, 0))
    full = lambda a: pl.BlockSpec(a.shape, lambda i: (0,) * a.ndim)
    outs = [(1024, BF16), (N_HEADS * KV_RANK, BF16), (512, BF16), (128, BF16), (128, BF16),
            (KV_RANK + LANES, BF16), (128, F32), (128, F32), (KV_RANK, F32), (ROPE_DIM, F32)]
    return pl.pallas_call(
        _qkv_kernel,
        grid=(m // tm,),
        in_specs=[row(D_MODEL), full(g), full(wa), full(wb), full(wuk), full(kvn),
                  row(LANES), row(LANES), row(LANES)],
        out_specs=[row(n) for n, _ in outs],
        out_shape=[jax.ShapeDtypeStruct((m, n), dt) for n, dt in outs],
        compiler_params=_cparams(("parallel",)),
        name="qkv",
    )(h, g, wa, wb, wuk, kvn, cos, sinm, cs)


def _t5_delta(dist, tab_ref, head):
    big = jnp.full(dist.shape, T5_EXACT, jnp.int32)
    for th in T5_THRESHOLDS:
        big = big + (dist >= th).astype(jnp.int32)
    bucket = jnp.where(dist < T5_EXACT, dist, big)
    last = tab_ref[(N_BUCKETS - 1) * N_HEADS + head]
    out = jnp.zeros(dist.shape, F32)
    for b in range(N_BUCKETS - 1):
        out = jnp.where(bucket == b, tab_ref[b * N_HEADS + head] - last, out)
    return out


def _bias_kernel(tab_ref, tile_ref, dec_ref):
    t = tile_ref.shape[1]
    r = lax.broadcasted_iota(jnp.int32, (t, 2 * t), 0)
    c = lax.broadcasted_iota(jnp.int32, (t, 2 * t), 1)
    dist = jnp.maximum(r + t - c, 0)
    cd = lax.broadcasted_iota(jnp.int32, (8, 2 * PAGE), 1)
    ddist = jnp.maximum(PAGE - cd, 0)
    for hd in range(N_HEADS):
        tile_ref[hd] = _t5_delta(dist, tab_ref, hd)
        row = _t5_delta(ddist, tab_ref, hd)[0:1]
        dec_ref[hd:hd + 1, :] = row
        dec_ref[N_HEADS + hd:N_HEADS + hd + 1, :] = row


def _bias_tiles(rel_bias, t):
    return pl.pallas_call(
        _bias_kernel,
        in_specs=[pl.BlockSpec(memory_space=pltpu.SMEM)],
        out_shape=[jax.ShapeDtypeStruct((N_HEADS, t, 2 * t), F32),
                   jax.ShapeDtypeStruct((2 * N_HEADS, 2 * PAGE), F32)],
        compiler_params=pltpu.CompilerParams(vmem_limit_bytes=VMEM_LIMIT),
        name="t5_bias",
    )(rel_bias.reshape(-1))


def _lambda(lam_ref, lam_init):
    a = jnp.sum(lam_ref[0:1, :] * lam_ref[1:2, :], axis=-1, keepdims=True)
    b = jnp.sum(lam_ref[2:3, :] * lam_ref[3:4, :], axis=-1, keepdims=True)
    return jnp.exp(a) - jnp.exp(b) + lam_init


def _attn_prompt_kernel(sc_ref, lam_ref, dq_ref, qlat_ref, qrope_ref, kb_ref, vb_ref, kcat_ref,
                        delta_ref, subln_ref, wuv_ref, oa_ref, ob_ref,
                        q2_ref, qm_ref, md_ref, ld_ref, accd_ref, mm_ref, lm_ref, accm_ref):
    t = ATTN_T
    i = pl.program_id(1)
    lane = lax.broadcasted_iota(jnp.int32, (t, LANES), 1)
    low = lane < DIFF_HEAD_DIM
    zero = jnp.zeros((t, LANES), BF16)
    for hd in range(N_HEADS):
        q = dq_ref[:, hd * LANES:(hd + 1) * LANES]
        q2_ref[2 * hd] = jnp.where(low, q, zero)
        q2_ref[2 * hd + 1] = jnp.where(low, zero, q)
        qm_ref[hd, :, 0:KV_RANK] = qlat_ref[:, hd * KV_RANK:(hd + 1) * KV_RANK]
        rp = qrope_ref[:, (hd // 2) * LANES:(hd // 2 + 1) * LANES]
        qm_ref[hd, :, KV_RANK:KV_RANK + LANES] = jnp.where(low, rp, zero) if hd % 2 == 0 else jnp.where(low, zero, rp)
    md_ref[...] = jnp.full(md_ref.shape, NEG_BIG, F32)
    mm_ref[...] = jnp.full(mm_ref.shape, NEG_BIG, F32)
    ld_ref[...] = jnp.zeros_like(ld_ref)
    lm_ref[...] = jnp.zeros_like(lm_ref)
    accd_ref[...] = jnp.zeros_like(accd_ref)
    accm_ref[...] = jnp.zeros_like(accm_ref)

    def update(s, idx, m_ref, l_ref, acc_ref, val):
        m_old = m_ref[idx]
        m_new = jnp.maximum(m_old, jnp.max(s, axis=-1, keepdims=True))
        alpha = jnp.exp(m_old - m_new)
        p = jnp.exp(s - m_new)
        l_ref[idx] = alpha * l_ref[idx] + jnp.sum(p, axis=-1, keepdims=True)
        acc_ref[idx] = alpha * acc_ref[idx] + _dot(p.astype(BF16), val)
        m_ref[idx] = m_new

    def block(ks, mode):
        kb = kb_ref[pl.ds(ks, t), :]
        vb = vb_ref[pl.ds(ks, t), :]
        kc = kcat_ref[pl.ds(ks, t), :]
        cv = kc[:, 0:KV_RANK]
        if mode == 2:
            rr = lax.broadcasted_iota(jnp.int32, (t, t), 0)
            cc = lax.broadcasted_iota(jnp.int32, (t, t), 1)
            keep = cc <= rr
        for hd in range(N_HEADS):
            if mode:
                dl = delta_ref[hd, :, (mode - 1) * t:mode * t]
            for mp in range(2):
                s = _dot_nt(q2_ref[2 * hd + mp], kb)
                if mode:
                    s = s + dl
                if mode == 2:
                    s = jnp.where(keep, s, NEG_BIG)
                update(s, 2 * hd + mp, md_ref, ld_ref, accd_ref, vb)
            s = _dot_nt(qm_ref[hd], kc) * MLA_SCALE
            if mode == 2:
                s = jnp.where(keep, s, NEG_BIG)
            update(s, hd, mm_ref, lm_ref, accm_ref, cv)

    def far_body(j, carry):
        block(pl.multiple_of(j * t, t), 0)
        return carry

    lax.fori_loop(0, jnp.maximum(i - 1, 0), far_body, 0)

    @pl.when(i > 0)
    def _():
        block(pl.multiple_of((i - 1) * t, t), 1)

    block(pl.multiple_of(i * t, t), 2)

    lam = _lambda(lam_ref, sc_ref[0])
    out_scale = sc_ref[1]
    sub = subln_ref[...]
    for hd in range(N_HEADS):
        o = accd_ref[2 * hd] / ld_ref[2 * hd] - lam * (accd_ref[2 * hd + 1] / ld_ref[2 * hd + 1])
        oa_ref[:, hd * LANES:(hd + 1) * LANES] = (_rms(o, sub) * out_scale).astype(BF16)
        ol = (accm_ref[hd] / lm_ref[hd]).astype(BF16)
        ob_ref[:, hd * LANES:(hd + 1) * LANES] = _dot(ol, wuv_ref[hd]).astype(BF16)


def _attn_prompt(sc, lamv, dq, qlat, qrope, kb, vb, kcat, delta, subln, wuv, nb, s):
    t = ATTN_T
    nq = s // t
    qrow = lambda n: pl.BlockSpec((t, n), lambda b, i: (b * nq + i, 0))
    krow = lambda n: pl.BlockSpec((s, n), lambda b, i: (b, 0))
    full = lambda a: pl.BlockSpec(a.shape, lambda b, i: (0,) * a.ndim)
    return pl.pallas_call(
        _attn_prompt_kernel,
        grid=(nb, nq),
        in_specs=[pl.BlockSpec(memory_space=pltpu.SMEM), full(lamv),
                  qrow(1024), qrow(N_HEADS * KV_RANK), qrow(512),
                  krow(128), krow(128), krow(KV_RANK + LANES),
                  full(delta), full(subln), full(wuv)],
        out_specs=[qrow(1024), qrow(1024)],
        out_shape=[jax.ShapeDtypeStruct((nb * s, 1024), BF16)] * 2,
        scratch_shapes=[
            pltpu.VMEM((2 * N_HEADS, t, LANES), BF16),
            pltpu.VMEM((N_HEADS, t, KV_RANK + LANES), BF16),
            pltpu.VMEM((2 * N_HEADS, t, 1), F32),
            pltpu.VMEM((2 * N_HEADS, t, 1), F32),
            pltpu.VMEM((2 * N_HEADS, t, DIFF_V), F32),
            pltpu.VMEM((N_HEADS, t, 1), F32),
            pltpu.VMEM((N_HEADS, t, 1), F32),
            pltpu.VMEM((N_HEADS, t, KV_RANK), F32),
        ],
        compiler_params=_cparams(("parallel", "arbitrary")),
        name="attn_prompt",
    )(sc, lamv, dq, qlat, qrope, kb, vb, kcat, delta, subln, wuv)


def _attn_sample_kernel(pt_ref, ly_ref, sc_ref, lam_ref, dq_ref, qlat_ref, qrope_ref,
                        ck_ref, cv_ref, cc_ref, cr_ref, nk_ref, nv_ref, nc_ref, nr_ref,
                        dec_ref, subln_ref, oa_ref, olat_ref,
                        md_ref, ld_ref, accd_ref, mm_ref, lm_ref, accm_ref):
    p = pl.program_id(1)
    last = pl.num_programs(1) - 1
    lane = lax.broadcasted_iota(jnp.int32, (N_HEADS, LANES), 1)
    q = dq_ref[...]
    zero = jnp.zeros_like(q)
    q2 = jnp.concatenate([jnp.where(lane < DIFF_HEAD_DIM, q, zero),
                          jnp.where(lane < DIFF_HEAD_DIM, zero, q)], axis=0).astype(BF16)
    ql = qlat_ref[...].astype(BF16)
    qr = qrope_ref[...].astype(BF16)

    @pl.when(p == 0)
    def _():
        md_ref[...] = jnp.full(md_ref.shape, NEG_BIG, F32)
        mm_ref[...] = jnp.full(mm_ref.shape, NEG_BIG, F32)
        ld_ref[...] = jnp.zeros_like(ld_ref)
        lm_ref[...] = jnp.zeros_like(lm_ref)
        accd_ref[...] = jnp.zeros_like(accd_ref)
        accm_ref[...] = jnp.zeros_like(accm_ref)

    def update(s, m_ref, l_ref, acc_ref, pv):
        m_old = m_ref[...]
        m_new = jnp.maximum(m_old, jnp.max(s, axis=-1, keepdims=True))
        alpha = jnp.exp(m_old - m_new)
        pr = jnp.exp(s - m_new)
        l_ref[...] = alpha * l_ref[...] + jnp.sum(pr, axis=-1, keepdims=True)
        acc_ref[...] = alpha * acc_ref[...] + pv(pr.astype(BF16))
        m_ref[...] = m_new

    kp = ck_ref[...].astype(BF16)
    vp = cv_ref[...].astype(BF16)
    cp = cc_ref[...].astype(BF16)
    rp = cr_ref[...].astype(BF16)
    sd = _dot_nt(q2, kp)
    sd = sd + jnp.where(p == last, dec_ref[:, 0:PAGE], 0.0)
    update(sd, md_ref, ld_ref, accd_ref, lambda pr: _dot(pr, vp))
    sm = (_dot_nt(ql, cp) + _dot_nt(qr, rp)) * MLA_SCALE
    update(sm, mm_ref, lm_ref, accm_ref, lambda pr: _dot(pr, cp))

    @pl.when(p == last)
    def _():
        nk = nk_ref[...].astype(BF16).astype(F32)
        nv = nv_ref[...].astype(BF16).astype(F32)
        nc = nc_ref[...].astype(BF16).astype(F32)
        nr = nr_ref[...].astype(BF16).astype(F32)
        s1 = jnp.sum(q2.astype(F32) * nk, axis=-1, keepdims=True) + dec_ref[:, PAGE:PAGE + 1]
        update(s1, md_ref, ld_ref, accd_ref, lambda pr: pr.astype(F32) * nv)
        s2 = (jnp.sum(ql.astype(F32) * nc, axis=-1, keepdims=True)
              + jnp.sum(qr.astype(F32) * nr, axis=-1, keepdims=True)) * MLA_SCALE
        update(s2, mm_ref, lm_ref, accm_ref, lambda pr: pr.astype(F32) * nc)
        lam = _lambda(lam_ref, sc_ref[0])
        od = accd_ref[...] / ld_ref[...]
        o = od[0:N_HEADS] - lam * od[N_HEADS:2 * N_HEADS]
        oa_ref[...] = _rms(o, subln_ref[...]) * sc_ref[1]
        olat_ref[...] = accm_ref[...] / lm_ref[...]


def _attn_sample(pt, ly, sc, lamv, dq, qlat, qrope, ck, cv, cc, cr, nk, nv, nc, nr, dec, subln):
    nseq, npages = pt.shape
    seq3 = lambda a: pl.BlockSpec((None,) + a.shape[1:], lambda b, p, pt, ly: (b, 0, 0))
    page = lambda a: pl.BlockSpec((None, None) + a.shape[2:],
                                  lambda b, p, pt, ly: (ly[0], pt[b * npages + p], 0, 0))
    full = lambda a: pl.BlockSpec(a.shape, lambda b, p, pt, ly: (0,) * a.ndim)
    gs = pltpu.PrefetchScalarGridSpec(
        num_scalar_prefetch=2,
        grid=(nseq, npages),
        in_specs=[pl.BlockSpec(memory_space=pltpu.SMEM), full(lamv),
                  seq3(dq), seq3(qlat), seq3(qrope),
                  page(ck), page(cv), page(cc), page(cr),
                  seq3(nk), seq3(nv), seq3(nc), seq3(nr),
                  full(dec), full(subln)],
        out_specs=[pl.BlockSpec((None, N_HEADS, DIFF_V), lambda b, p, pt, ly: (b, 0, 0)),
                   pl.BlockSpec((None, N_HEADS, KV_RANK), lambda b, p, pt, ly: (b, 0, 0))],
        scratch_shapes=[
            pltpu.VMEM((2 * N_HEADS, 1), F32), pltpu.VMEM((2 * N_HEADS, 1), F32),
            pltpu.VMEM((2 * N_HEADS, DIFF_V), F32),
            pltpu.VMEM((N_HEADS, 1), F32), pltpu.VMEM((N_HEADS, 1), F32),
            pltpu.VMEM((N_HEADS, KV_RANK), F32),
        ],
    )
    return pl.pallas_call(
        _attn_sample_kernel,
        grid_spec=gs,
        out_shape=[jax.ShapeDtypeStruct((nseq, N_HEADS, DIFF_V), F32),
                   jax.ShapeDtypeStruct((nseq, N_HEADS, KV_RANK), F32)],
        compiler_params=_cparams(("parallel", "arbitrary")),
        name="attn_sample",
    )(pt.reshape(-1), ly, sc, lamv, dq, qlat, qrope, ck, cv, cc, cr, nk, nv, nc, nr, dec, subln)


def _uv_kernel(olat_ref, wuv_ref, ob_ref):
    for hd in range(N_HEADS):
        ob_ref[:, hd * LANES:(hd + 1) * LANES] = _dot(
            olat_ref[:, hd * KV_RANK:(hd + 1) * KV_RANK].astype(BF16), wuv_ref[hd]).astype(BF16)


def _uv_proj(olat, wuv):
    m = olat.shape[0]
    return pl.pallas_call(
        _uv_kernel,
        out_shape=jax.ShapeDtypeStruct((m, N_HEADS * MLA_V), BF16),
        compiler_params=pltpu.CompilerParams(vmem_limit_bytes=VMEM_LIMIT),
        name="uv_proj",
    )(olat, wuv)


MERGE_TN = 512


def _merge_kernel(h_ref, g_ref, oa_ref, ob_ref, wga_ref, wgb_ref, wba_ref, wbb_ref, wo_ref, o_ref,
                  u_ref, acc_ref):
    j = pl.program_id(1)

    @pl.when(j == 0)
    def _():
        u_ref[...] = _rms(h_ref[...], g_ref[...]).astype(BF16)
        acc_ref[...] = jnp.zeros_like(acc_ref)

    u = u_ref[...]
    ga = _dot(u, wga_ref[...])
    gb = _dot(u, wgb_ref[...])
    a = _dot(oa_ref[...], wba_ref[...])
    b = _dot(ob_ref[...], wbb_ref[...])
    mg = (jax.nn.sigmoid(ga) * a + jax.nn.sigmoid(gb) * b).astype(BF16)
    acc_ref[...] += _dot(mg, wo_ref[...])

    @pl.when(j == pl.num_programs(1) - 1)
    def _():
        o_ref[...] = h_ref[...] + acc_ref[...]


def _merge(h, g, oa, ob, wga, wgb, wba, wbb, wo, tm):
    m = h.shape[0]
    tn = MERGE_TN
    return pl.pallas_call(
        _merge_kernel,
        grid=(m // tm, D_MODEL // tn),
        in_specs=[
            pl.BlockSpec((tm, D_MODEL), lambda i, j: (i, 0)),
            pl.BlockSpec((1, D_MODEL), lambda i, j: (0, 0)),
            pl.BlockSpec((tm, 1024), lambda i, j: (i, 0)),
            pl.BlockSpec((tm, 1024), lambda i, j: (i, 0)),
            pl.BlockSpec((D_MODEL, tn), lambda i, j: (0, j)),
            pl.BlockSpec((D_MODEL, tn), lambda i, j: (0, j)),
            pl.BlockSpec((1024, tn), lambda i, j: (0, j)),
            pl.BlockSpec((1024, tn), lambda i, j: (0, j)),
            pl.BlockSpec((tn, D_MODEL), lambda i, j: (j, 0)),
        ],
        out_specs=pl.BlockSpec((tm, D_MODEL), lambda i, j: (i, 0)),
        out_shape=jax.ShapeDtypeStruct((m, D_MODEL), F32),
        scratch_shapes=[pltpu.VMEM((tm, D_MODEL), BF16), pltpu.VMEM((tm, D_MODEL), F32)],
        compiler_params=_cparams(("parallel", "arbitrary")),
        name="merge",
    )(h, g, oa, ob, wga, wgb, wba, wbb, wo)


PLE_TN = 512


def _ple_kernel(h_ref, hc_ref, g_ref, p_ref, wpg_ref, wpp_ref, o_ref, hn_ref):
    @pl.when(pl.program_id(1) == 0)
    def _():
        hn_ref[...] = _rms(h_ref[...], g_ref[...]).astype(BF16)

    gate = jax.nn.sigmoid(_dot(hn_ref[...], wpg_ref[...]))
    proj = _dot(p_ref[...].astype(BF16), wpp_ref[...])
    o_ref[...] = hc_ref[...] + gate * proj


def _ple(h, g, p, wpg, wpp, tm):
    m = h.shape[0]
    tn = PLE_TN
    return pl.pallas_call(
        _ple_kernel,
        grid=(m // tm, D_MODEL // tn),
        in_specs=[
            pl.BlockSpec((tm, D_MODEL), lambda i, j: (i, 0)),
            pl.BlockSpec((tm, tn), lambda i, j: (i, j)),
            pl.BlockSpec((1, D_MODEL), lambda i, j: (0, 0)),
            pl.BlockSpec((tm, PLE_DIM), lambda i, j: (i, 0)),
            pl.BlockSpec((D_MODEL, tn), lambda i, j: (0, j)),
            pl.BlockSpec((PLE_DIM, tn), lambda i, j: (0, j)),
        ],
        out_specs=pl.BlockSpec((tm, tn), lambda i, j: (i, j)),
        out_shape=jax.ShapeDtypeStruct((m, D_MODEL), F32),
        scratch_shapes=[pltpu.VMEM((tm, D_MODEL), BF16)],
        compiler_params=_cparams(("parallel", "arbitrary")),
        name="ple",
    )(h, h, g, p, wpg, wpp)


def _norm_kernel(x_ref, g_ref, o_ref):
    o_ref[...] = _rms(x_ref[...], g_ref[...])


def _final_norm(x, g, tm):
    m = x.shape[0]
    return pl.pallas_call(
        _norm_kernel,
        grid=(m // tm,),
        in_specs=[pl.BlockSpec((tm, D_MODEL), lambda i: (i, 0)), pl.BlockSpec((1, D_MODEL), lambda i: (0, 0))],
        out_specs=pl.BlockSpec((tm, D_MODEL), lambda i: (i, 0)),
        out_shape=jax.ShapeDtypeStruct((m, D_MODEL), F32),
        compiler_params=_cparams(("parallel",)),
        name="final_norm",
    )(x, g)


def _swap_halves(w):
    half = ROPE_DIM // 2
    return jnp.concatenate([w[:, half:], w[:, :half]], axis=1)


def _prep_layer(w_ffn_in, w_ffn_out, w_in, w_uk, w_uv):
    pad_c = lambda w: jnp.pad(w, ((0, 0), (0, D_FF_PAD - D_FF))).astype(BF16)
    wg = pad_c(w_ffn_in[:, :D_FF])
    wu = pad_c(w_ffn_in[:, D_FF:])
    wo = jnp.pad(w_ffn_out, ((0, D_FF_PAD - D_FF), (0, 0))).astype(BF16)
    out = {"ffn": (wg, wu, wo)}
    if w_in is not None:
        per = MLA_NOPE + ROPE_DIM
        nope = [w_in[:, _OFF_MQ + h * per:_OFF_MQ + h * per + MLA_NOPE] for h in range(N_HEADS)]
        rope = [w_in[:, _OFF_MQ + h * per + MLA_NOPE:_OFF_MQ + (h + 1) * per] for h in range(N_HEADS)]
        kr = w_in[:, _OFF_KR:_OFF_KR + ROPE_DIM]
        wa = jnp.concatenate([w_in[:, _OFF_DQ:_OFF_DQ + 1024]] + nope + rope + [_swap_halves(r) for r in rope],
                             axis=1).astype(BF16)
        wb = jnp.concatenate([w_in[:, _OFF_DK:_OFF_DK + 256], w_in[:, _OFF_CKV:_OFF_CKV + KV_RANK],
                              kr, _swap_halves(kr)], axis=1).astype(BF16)
        out["wa"] = wa
        out["wb"] = wb
        out["wga"] = w_in[:, _OFF_GA:_OFF_GA + D_MODEL].astype(BF16)
        out["wgb"] = w_in[:, _OFF_GB:_OFF_GB + D_MODEL].astype(BF16)
        out["wuk"] = jnp.transpose(w_uk, (1, 2, 0)).astype(BF16)
        out["wuv"] = jnp.transpose(w_uv, (1, 0, 2)).astype(BF16)
    return out


def _rope_tables(pos):
    half = ROPE_DIM // 2
    inv = ROPE_THETA ** (-jnp.arange(half, dtype=F32) / half)
    ang = pos.astype(F32)[:, None] * inv[None, :]
    cos = jnp.cos(ang)
    sin = jnp.sin(ang)
    cos128 = jnp.concatenate([cos, cos, cos, cos], axis=1)
    sinm128 = jnp.concatenate([-sin, sin, -sin, sin], axis=1)
    cs128 = jnp.concatenate([cos, cos, -sin, sin], axis=1)
    return cos128, sinm128, cs128


def _tile_m(m):
    return min(m, 512)


def kernel(x_prompt, x_sample, cache_diff_k, cache_diff_v, cache_mla_ckv, cache_mla_krope, page_table,
           p_prompt, p_sample, rel_bias, norm_ffn1, w_ffn1_in, w_ffn1_out, norm_mix, w_in,
           diff_lambda_q1, diff_lambda_k1, diff_lambda_q2, diff_lambda_k2, diff_subln, mla_kv_norm,
           mla_w_uk, mla_w_uv, w_branch_a, w_branch_b, w_out, norm_ffn2, w_ffn2_in, w_ffn2_out,
           norm_ple, w_ple_gate, w_ple_proj, norm_final):
    nb, s, _ = x_prompt.shape
    nseq = x_sample.shape[0]
    assert x_sample.shape[1] == 1 and cache_diff_k.shape[2] == PAGE and s % ATTN_T == 0
    past_len = page_table.shape[1] * PAGE
    mp = nb * s
    hp = x_prompt.reshape(mp, D_MODEL)
    hs = x_sample.reshape(nseq, D_MODEL)
    tmp, tms = _tile_m(mp), _tile_m(nseq)
    tqp, tqs = min(mp, QKV_TM), min(nseq, QKV_TM)

    rope_p = _rope_tables(jnp.tile(jnp.arange(s, dtype=jnp.int32), nb))
    rope_s = _rope_tables(jnp.full((nseq,), past_len, jnp.int32))
    delta, dec = _bias_tiles(rel_bias, ATTN_T)
    row = lambda v: v.reshape(1, -1)

    rows_p = ([], [], [], [])
    rows_s = ([], [], [], [])
    for i in range(DEPTH):
        w1 = _prep_layer(w_ffn1_in[i], w_ffn1_out[i], w_in[i], mla_w_uk[i], mla_w_uv[i])
        w2 = _prep_layer(w_ffn2_in[i], w_ffn2_out[i], None, None, None)
        wba = w_branch_a[i].astype(BF16)
        wbb = w_branch_b[i].astype(BF16)
        wout = w_out[i].astype(BF16)
        wpg = w_ple_gate[i].astype(BF16)
        wpp = w_ple_proj[i].astype(BF16)
        lam_init = 0.8 - 0.6 * math.exp(-0.3 * i)
        sc = jnp.array([lam_init, 1.0 - lam_init], F32)
        lamv = jnp.stack([diff_lambda_q1[i], diff_lambda_k1[i], diff_lambda_q2[i], diff_lambda_k2[i]])
        subln = row(diff_subln[i])
        kvn = row(mla_kv_norm[i])
        ly = jnp.array([i], jnp.int32)

        hp = _ffn(hp, row(norm_ffn1[i]), *w1["ffn"], tmp)
        (dq, qlat, qrope, kb, vb, kcat, dk, dv, ckv, kr) = _qkv(
            hp, row(norm_mix[i]), w1["wa"], w1["wb"], w1["wuk"], kvn, *rope_p, tqp)
        oa, ob = _attn_prompt(sc, lamv, dq, qlat, qrope, kb, vb, kcat, delta, subln, w1["wuv"], nb, s)
        hp = _merge(hp, row(norm_mix[i]), oa, ob, w1["wga"], w1["wgb"], wba, wbb, wout, tmp)
        hp = _ffn(hp, row(norm_ffn2[i]), *w2["ffn"], tmp)
        hp = _ple(hp, row(norm_ple[i]), p_prompt[i].reshape(mp, PLE_DIM), wpg, wpp, tmp)
        for lst, r in zip(rows_p, (dk, dv, ckv, kr)):
            lst.append(r.reshape(nb, s, -1))

        hs = _ffn(hs, row(norm_ffn1[i]), *w1["ffn"], tms)
        (dq, qlat, qrope, kb, vb, kcat, dk, dv, ckv, kr) = _qkv(
            hs, row(norm_mix[i]), w1["wa"], w1["wb"], w1["wuk"], kvn, *rope_s, tqs)
        oa3, olat3 = _attn_sample(
            page_table, ly, sc, lamv,
            dq.astype(F32).reshape(nseq, N_HEADS, DIFF_QK), qlat.astype(F32).reshape(nseq, N_HEADS, KV_RANK),
            qrope.astype(F32).reshape(nseq, N_HEADS, ROPE_DIM),
            cache_diff_k, cache_diff_v, cache_mla_ckv, cache_mla_krope,
            dk.reshape(nseq, 1, -1), dv.reshape(nseq, 1, -1), ckv.reshape(nseq, 1, -1), kr.reshape(nseq, 1, -1),
            dec, subln)
        ob = _uv_proj(olat3.reshape(nseq, N_HEADS * KV_RANK), w1["wuv"])
        hs = _merge(hs, row(norm_mix[i]), oa3.reshape(nseq, N_HEADS * DIFF_V).astype(BF16), ob,
                    w1["wga"], w1["wgb"], wba, wbb, wout, tms)
        hs = _ffn(hs, row(norm_ffn2[i]), *w2["ffn"], tms)
        hs = _ple(hs, row(norm_ple[i]), p_sample[i].reshape(nseq, PLE_DIM), wpg, wpp, tms)
        for lst, r in zip(rows_s, (dk, dv, ckv, kr)):
            lst.append(r.reshape(nseq, 1, -1))

    y_prompt = _final_norm(hp, row(norm_final), tmp).reshape(nb, s, D_MODEL)
    y_sample = _final_norm(hs, row(norm_final), tms).reshape(nseq, 1, D_MODEL)
    return (y_prompt, y_sample,
            jnp.stack(rows_p[0]), jnp.stack(rows_p[1]), jnp.stack(rows_p[2]), jnp.stack(rows_p[3]),
            jnp.stack(rows_s[0]), jnp.stack(rows_s[1]), jnp.stack(rows_s[2]), jnp.stack(rows_s[3]))
```

```python
import functools
import math

import jax
import jax.numpy as jnp
from jax import lax
from jax.experimental import pallas as pl
from jax.experimental.pallas import tpu as pltpu

F32 = jnp.float32
BF16 = jnp.bfloat16

D_MODEL = 2048
DEPTH = 4
PAGE = 128
N_HEADS = 8
DIFF_HEAD_DIM = 64
DIFF_QK = 128
DIFF_V = 128
MLA_NOPE = 128
ROPE_DIM = 64
MLA_V = 128
KV_RANK = 256
ROPE_THETA = 10000.0
N_BUCKETS = 32
T5_EXACT = 16
T5_MAX_DIST = 128
D_FF = 5504
PLE_DIM = 256
EPS = 1e-6
DIFF_SCALE = DIFF_HEAD_DIM ** -0.5
MLA_SCALE = (MLA_NOPE + ROPE_DIM) ** -0.5
LOG2E = math.log2(math.e)

LANES = 128
D_FF_PAD = 5632
FFN_TF = 512
ATTN_T = 256
MAX_CHUNK_PAGES = 16
NEG_BIG = -1e30
VMEM_LIMIT = 56 * 1024 * 1024

_OFF_DQ = 0
_OFF_DK = 1024
_OFF_DV = 1152
_OFF_MQ = 1280
_OFF_CKV = 2816
_OFF_KR = 3072
_OFF_GA = 3136
_OFF_GB = 5184


def _t5_thresholds():
    out = []
    for k in range(1, N_BUCKETS - T5_EXACT):
        n = T5_EXACT
        while int(math.log(n / T5_EXACT) / math.log(T5_MAX_DIST / T5_EXACT) * (N_BUCKETS - T5_EXACT)) < k:
            n += 1
        out.append(n)
    return tuple(out)


T5_THRESHOLDS = _t5_thresholds()
T5_FAR = T5_THRESHOLDS[-1]
assert T5_FAR <= PAGE and T5_FAR <= ATTN_T


def _cparams(sem):
    return pltpu.CompilerParams(dimension_semantics=sem, vmem_limit_bytes=VMEM_LIMIT)


def _rms(x, g):
    ms = jnp.mean(x * x, axis=-1, keepdims=True)
    return (x * lax.rsqrt(ms + EPS)) * g


def _dot(a, b):
    return jnp.dot(a, b, preferred_element_type=F32)


def _dot_nt(a, b):
    return lax.dot_general(a, b, (((1,), (1,)), ((), ())), preferred_element_type=F32)


def _ffn_kernel(x_ref, g_ref, wg_ref, wu_ref, wo_ref, o_ref, xn_ref, acc_ref):
    j = pl.program_id(1)

    @pl.when(j == 0)
    def _():
        xn_ref[...] = _rms(x_ref[...], g_ref[...]).astype(BF16)
        acc_ref[...] = jnp.zeros_like(acc_ref)

    xn = xn_ref[...]
    g = _dot(xn, wg_ref[...])
    u = _dot(xn, wu_ref[...])
    a = (g * jax.nn.sigmoid(g) * u).astype(BF16)
    acc_ref[...] += _dot(a, wo_ref[...])

    @pl.when(j == pl.num_programs(1) - 1)
    def _():
        o_ref[...] = x_ref[...] + 0.5 * acc_ref[...]


def _ffn(x, g, wg, wu, wo, tm):
    m = x.shape[0]
    nj = D_FF_PAD // FFN_TF
    return pl.pallas_call(
        _ffn_kernel,
        grid=(m // tm, nj),
        in_specs=[
            pl.BlockSpec((tm, D_MODEL), lambda i, j: (i, 0)),
            pl.BlockSpec((1, D_MODEL), lambda i, j: (0, 0)),
            pl.BlockSpec((D_MODEL, FFN_TF), lambda i, j: (0, j)),
            pl.BlockSpec((D_MODEL, FFN_TF), lambda i, j: (0, j)),
            pl.BlockSpec((FFN_TF, D_MODEL), lambda i, j: (j, 0)),
        ],
        out_specs=pl.BlockSpec((tm, D_MODEL), lambda i, j: (i, 0)),
        out_shape=jax.ShapeDtypeStruct((m, D_MODEL), F32),
        scratch_shapes=[pltpu.VMEM((tm, D_MODEL), BF16), pltpu.VMEM((tm, D_MODEL), F32)],
        compiler_params=_cparams(("parallel", "arbitrary")),
        name="ffn",
    )(x, g, wg, wu, wo)


def _qkv_kernel(prompt, h_ref, g_ref, wa_ref, wb_ref, wuk_ref, kvn_ref, cos_ref, sinm_ref, cs_ref,
                dq_ref, qlat_ref, qrope_ref, dk_ref, dv_ref, ckv_ref, kr_ref, *kv_refs):
    qdt = dq_ref.dtype
    u = _rms(h_ref[...], g_ref[...]).astype(BF16)
    dq_ref[...] = (_dot(u, wa_ref[:, 0:1024]) * DIFF_SCALE).astype(BF16).astype(qdt)
    for hd in range(N_HEADS):
        qn = _dot(u, wa_ref[:, 1024 + hd * MLA_NOPE:1024 + (hd + 1) * MLA_NOPE]).astype(BF16)
        qlat_ref[:, hd * KV_RANK:(hd + 1) * KV_RANK] = _dot(qn, wuk_ref[hd]).astype(BF16).astype(qdt)
    cos = cos_ref[...]
    sinm = sinm_ref[...]
    for gi in range(4):
        y = _dot(u, wa_ref[:, 2048 + gi * LANES:2048 + (gi + 1) * LANES])
        ys = _dot(u, wa_ref[:, 2560 + gi * LANES:2560 + (gi + 1) * LANES])
        qrope_ref[:, gi * LANES:(gi + 1) * LANES] = (y * cos + ys * sinm).astype(BF16).astype(qdt)
    kv = _dot(u, wb_ref[...])
    dk = kv[:, 0:128]
    dv = kv[:, 128:256]
    dk_ref[...] = dk
    dv_ref[...] = dv
    ckv = _rms(kv[:, 256:512], kvn_ref[...])
    ckv_ref[...] = ckv
    t = kv[:, 512:640] * cs_ref[...]
    kr2 = t + pltpu.roll(t, 64, 1)
    kr_ref[...] = kr2[:, 0:ROPE_DIM]
    if prompt:
        kb_ref, vt_ref, kcat_ref, ct_ref = kv_refs
        kb_ref[...] = dk.astype(BF16)
        vt_ref[0] = dv.T.astype(BF16)
        kcat_ref[:, 0:KV_RANK] = ckv.astype(BF16)
        kcat_ref[:, KV_RANK:KV_RANK + LANES] = kr2.astype(BF16)
        ct_ref[0] = ckv.T.astype(BF16)


def _qkv(prompt, h, g, wa, wb, wuk, kvn, cos, sinm, cs, tm):
    m = h.shape[0]
    row = lambda n: pl.BlockSpec((tm, n), lambda i: (i, 0))
    full = lambda a: pl.BlockSpec(a.shape, lambda i: (0,) * a.ndim)
    qdt = BF16 if prompt else F32
    outs = [(1024, qdt), (N_HEADS * KV_RANK, qdt), (512, qdt),
            (128, F32), (128, F32), (KV_RANK, F32), (ROPE_DIM, F32)]
    out_specs = [row(n) for n, _ in outs]
    out_shape = [jax.ShapeDtypeStruct((m, n), dt) for n, dt in outs]
    if prompt:
        assert tm == ATTN_T
        tr = lambda n: pl.BlockSpec((1, n, tm), lambda i: (i, 0, 0))
        out_specs += [row(128), tr(DIFF_V), row(KV_RANK + LANES), tr(KV_RANK)]
        out_shape += [jax.ShapeDtypeStruct((m, 128), BF16),
                      jax.ShapeDtypeStruct((m // tm, DIFF_V, tm), BF16),
                      jax.ShapeDtypeStruct((m, KV_RANK + LANES), BF16),
                      jax.ShapeDtypeStruct((m // tm, KV_RANK, tm), BF16)]
    return pl.pallas_call(
        functools.partial(_qkv_kernel, prompt),
        grid=(m // tm,),
        in_specs=[row(D_MODEL), full(g), full(wa), full(wb), full(wuk), full(kvn),
                  row(LANES), row(LANES), row(LANES)],
        out_specs=out_specs,
        out_shape=out_shape,
        compiler_params=_cparams(("parallel",)),
        name="qkv_prompt" if prompt else "qkv_sample",
    )(h, g, wa, wb, wuk, kvn, cos, sinm, cs)


def _t5_delta(dist, tab_ref, head):
    big = jnp.full(dist.shape, T5_EXACT, jnp.int32)
    for th in T5_THRESHOLDS:
        big = big + (dist >= th).astype(jnp.int32)
    bucket = jnp.where(dist < T5_EXACT, dist, big)
    last = tab_ref[(N_BUCKETS - 1) * N_HEADS + head]
    out = jnp.zeros(dist.shape, F32)
    for b in range(N_BUCKETS - 1):
        out = jnp.where(bucket == b, tab_ref[b * N_HEADS + head] - last, out)
    return out


def _bias_kernel(tab_ref, tile_ref, dec_ref):
    t = tile_ref.shape[2]
    r = lax.broadcasted_iota(jnp.int32, (2 * t, t), 0)
    c = lax.broadcasted_iota(jnp.int32, (2 * t, t), 1)
    dist = jnp.maximum(c + t - r, 0)
    cd = lax.broadcasted_iota(jnp.int32, (8, 2 * PAGE), 1)
    ddist = jnp.maximum(PAGE - cd, 0)
    for hd in range(N_HEADS):
        tile_ref[hd] = _t5_delta(dist, tab_ref, hd)
        row = _t5_delta(ddist, tab_ref, hd)[0:1]
        dec_ref[hd:hd + 1, :] = row
        dec_ref[N_HEADS + hd:N_HEADS + hd + 1, :] = row


def _bias_tiles(rel_bias, t):
    return pl.pallas_call(
        _bias_kernel,
        in_specs=[pl.BlockSpec(memory_space=pltpu.SMEM)],
        out_shape=[jax.ShapeDtypeStruct((N_HEADS, 2 * t, t), F32),
                   jax.ShapeDtypeStruct((2 * N_HEADS, 2 * PAGE), F32)],
        compiler_params=pltpu.CompilerParams(vmem_limit_bytes=VMEM_LIMIT),
        name="t5_bias",
    )(rel_bias.reshape(-1))


def _lambda(lam_ref, lam_init):
    a = jnp.sum(lam_ref[0:1, :] * lam_ref[1:2, :], axis=-1, keepdims=True)
    b = jnp.sum(lam_ref[2:3, :] * lam_ref[3:4, :], axis=-1, keepdims=True)
    return jnp.exp(a) - jnp.exp(b) + lam_init


def _attn_prompt_kernel(sc_ref, lam_ref, dq_ref, qlat_ref, qrope_ref, kb_ref, vt_ref, kcat_ref, ct_ref,
                        delta_ref, subln_ref, wuvt_ref, oa_ref, ob_ref,
                        q2_ref, qm_ref, md_ref, ld_ref, accd_ref, mm_ref, lm_ref, accm_ref):
    t = ATTN_T
    i = pl.program_id(1)
    lane = lax.broadcasted_iota(jnp.int32, (t, LANES), 1)
    low = lane < DIFF_HEAD_DIM
    zero = jnp.zeros((t, LANES), BF16)
    for hd in range(N_HEADS):
        q = dq_ref[:, hd * LANES:(hd + 1) * LANES]
        q2_ref[2 * hd] = jnp.where(low, q, zero)
        q2_ref[2 * hd + 1] = jnp.where(low, zero, q)
        qm_ref[hd, :, 0:KV_RANK] = qlat_ref[:, hd * KV_RANK:(hd + 1) * KV_RANK]
        rp = qrope_ref[:, (hd // 2) * LANES:(hd // 2 + 1) * LANES]
        qm_ref[hd, :, KV_RANK:KV_RANK + LANES] = jnp.where(low, rp, zero) if hd % 2 == 0 else jnp.where(low, zero, rp)
    md_ref[...] = jnp.full(md_ref.shape, NEG_BIG, F32)
    mm_ref[...] = jnp.full(mm_ref.shape, NEG_BIG, F32)
    ld_ref[...] = jnp.zeros_like(ld_ref)
    lm_ref[...] = jnp.zeros_like(lm_ref)
    accd_ref[...] = jnp.zeros_like(accd_ref)
    accm_ref[...] = jnp.zeros_like(accm_ref)

    def update(st, idx, m_ref, l_ref, acc_ref, val_t, c):
        m_old = m_ref[idx]
        m_new = jnp.maximum(m_old, jnp.max(st, axis=0, keepdims=True))
        alpha = jnp.exp2((m_old - m_new) * c)
        p = jnp.exp2((st - m_new) * c)
        l_ref[idx] = alpha * l_ref[idx] + jnp.sum(p, axis=0, keepdims=True)
        acc_ref[idx] = alpha * acc_ref[idx] + _dot(val_t, p.astype(BF16))
        m_ref[idx] = m_new

    def block(j, mode):
        ks = pl.multiple_of(j * t, t)
        kb = kb_ref[pl.ds(ks, t), :]
        kc = kcat_ref[pl.ds(ks, t), :]
        vt = vt_ref[j]
        ct = ct_ref[j]
        if mode == 2:
            kk = lax.broadcasted_iota(jnp.int32, (t, t), 0)
            qq = lax.broadcasted_iota(jnp.int32, (t, t), 1)
            keep = kk <= qq
        for hd in range(N_HEADS):
            if mode:
                dl = delta_ref[hd, (mode - 1) * t:mode * t, :]
            for mp in range(2):
                st = _dot_nt(kb, q2_ref[2 * hd + mp])
                if mode:
                    st = st + dl
                if mode == 2:
                    st = jnp.where(keep, st, NEG_BIG)
                update(st, 2 * hd + mp, md_ref, ld_ref, accd_ref, vt, LOG2E)
            st = _dot_nt(kc, qm_ref[hd])
            if mode == 2:
                st = jnp.where(keep, st, NEG_BIG)
            update(st, hd, mm_ref, lm_ref, accm_ref, ct, MLA_SCALE * LOG2E)

    def far_body(j, carry):
        block(j, 0)
        return carry

    lax.fori_loop(0, jnp.maximum(i - 1, 0), far_body, 0)

    @pl.when(i > 0)
    def _():
        block(i - 1, 1)

    block(i, 2)

    lam = _lambda(lam_ref, sc_ref[0])
    out_scale = sc_ref[1]
    sub = subln_ref[...]
    for hd in range(N_HEADS):
        o = accd_ref[2 * hd] / ld_ref[2 * hd] - lam * (accd_ref[2 * hd + 1] / ld_ref[2 * hd + 1])
        ms = jnp.mean(o * o, axis=0, keepdims=True)
        y = (o * lax.rsqrt(ms + EPS)) * sub * out_scale
        oa_ref[:, hd * LANES:(hd + 1) * LANES] = y.T.astype(BF16)
        ol = (accm_ref[hd] / lm_ref[hd]).astype(BF16)
        ob_ref[:, hd * LANES:(hd + 1) * LANES] = _dot(wuvt_ref[hd], ol).T.astype(BF16)


def _attn_prompt(sc, lamv, dq, qlat, qrope, kb, vt, kcat, ct, delta, subln_col, wuvt, nb, s):
    t = ATTN_T
    nq = s // t
    qrow = lambda n: pl.BlockSpec((t, n), lambda b, i: (b * nq + i, 0))
    krow = lambda n: pl.BlockSpec((s, n), lambda b, i: (b, 0))
    ktr = lambda n: pl.BlockSpec((nq, n, t), lambda b, i: (b, 0, 0))
    full = lambda a: pl.BlockSpec(a.shape, lambda b, i: (0,) * a.ndim)
    return pl.pallas_call(
        _attn_prompt_kernel,
        grid=(nb, nq),
        in_specs=[pl.BlockSpec(memory_space=pltpu.SMEM), full(lamv),
                  qrow(1024), qrow(N_HEADS * KV_RANK), qrow(512),
                  krow(128), ktr(DIFF_V), krow(KV_RANK + LANES), ktr(KV_RANK),
                  full(delta), full(subln_col), full(wuvt)],
        out_specs=[qrow(1024), qrow(1024)],
        out_shape=[jax.ShapeDtypeStruct((nb * s, 1024), BF16)] * 2,
        scratch_shapes=[
            pltpu.VMEM((2 * N_HEADS, t, LANES), BF16),
            pltpu.VMEM((N_HEADS, t, KV_RANK + LANES), BF16),
            pltpu.VMEM((2 * N_HEADS, 1, t), F32),
            pltpu.VMEM((2 * N_HEADS, 1, t), F32),
            pltpu.VMEM((2 * N_HEADS, DIFF_V, t), F32),
            pltpu.VMEM((N_HEADS, 1, t), F32),
            pltpu.VMEM((N_HEADS, 1, t), F32),
            pltpu.VMEM((N_HEADS, KV_RANK, t), F32),
        ],
        compiler_params=_cparams(("parallel", "arbitrary")),
        name="attn_prompt",
    )(sc, lamv, dq, qlat, qrope, kb, vt, kcat, ct, delta, subln_col, wuvt)


def _attn_sample_kernel(cpp, nchunk, pt_ref, ly_ref, sc_ref, lam_ref, dq_ref, qlat_ref, qrope_ref,
                        ck_hbm, cv_hbm, cc_hbm, cr_hbm, nk_ref, nv_ref, nc_ref, nr_ref,
                        dec_ref, subln_ref, oa_ref, olat_ref,
                        kbuf, vbuf, cbuf, rbuf, sem, md_ref, ld_ref, accd_ref, mm_ref, lm_ref, accm_ref):
    b = pl.program_id(0)
    nseq = pl.num_programs(0)
    ly = ly_ref[0]
    n = cpp * PAGE

    def slot_of(seq, c):
        if nchunk % 2 == 0:
            return c % 2
        return lax.rem(seq * nchunk + c, 2)

    def copies(seq, c, slot):
        out = []
        for j in range(cpp):
            pg = pt_ref[(seq * nchunk + c) * cpp + j]
            rows = pl.ds(j * PAGE, PAGE)
            out.append(pltpu.make_async_copy(ck_hbm.at[ly, pg], kbuf.at[slot, rows, :], sem.at[0, slot]))
            out.append(pltpu.make_async_copy(cv_hbm.at[ly, pg], vbuf.at[slot, rows, :], sem.at[1, slot]))
            out.append(pltpu.make_async_copy(cc_hbm.at[ly, pg], cbuf.at[slot, rows, :], sem.at[2, slot]))
            out.append(pltpu.make_async_copy(cr_hbm.at[ly, pg], rbuf.at[slot, j], sem.at[3, slot]))
        return out

    def start(seq, c):
        for cp in copies(seq, c, slot_of(seq, c)):
            cp.start()

    @pl.when(b == 0)
    def _():
        start(b, 0)

    lane = lax.broadcasted_iota(jnp.int32, (N_HEADS, LANES), 1)
    q = dq_ref[...]
    zero = jnp.zeros_like(q)
    q2 = jnp.concatenate([jnp.where(lane < DIFF_HEAD_DIM, q, zero),
                          jnp.where(lane < DIFF_HEAD_DIM, zero, q)], axis=0).astype(BF16)
    ql = qlat_ref[...].astype(BF16)
    qr = qrope_ref[...].astype(BF16)

    md_ref[...] = jnp.full(md_ref.shape, NEG_BIG, F32)
    mm_ref[...] = jnp.full(mm_ref.shape, NEG_BIG, F32)
    ld_ref[...] = jnp.zeros_like(ld_ref)
    lm_ref[...] = jnp.zeros_like(lm_ref)
    accd_ref[...] = jnp.zeros_like(accd_ref)
    accm_ref[...] = jnp.zeros_like(accm_ref)

    def update(s, m_ref, l_ref, acc_ref, pv, c):
        m_old = m_ref[...]
        m_new = jnp.maximum(m_old, jnp.max(s, axis=-1, keepdims=True))
        alpha = jnp.exp2((m_old - m_new) * c)
        pr = jnp.exp2((s - m_new) * c)
        l_ref[...] = alpha * l_ref[...] + jnp.sum(pr, axis=-1, keepdims=True)
        acc_ref[...] = alpha * acc_ref[...] + pv(pr.astype(BF16))
        m_ref[...] = m_new

    for c in range(nchunk):
        slot = slot_of(b, c)
        if c + 1 < nchunk:
            start(b, c + 1)
        else:
            @pl.when(b + 1 < nseq)
            def _():
                start(b + 1, 0)
        for cp in copies(b, c, slot):
            cp.wait()
        kp = kbuf[slot].astype(BF16)
        vp = vbuf[slot].astype(BF16)
        cp_ = cbuf[slot].astype(BF16)
        sd = _dot_nt(q2, kp)
        if c == nchunk - 1:
            tail = dec_ref[:, 0:PAGE]
            sd = sd + (tail if cpp == 1 else jnp.concatenate([jnp.zeros((2 * N_HEADS, n - PAGE), F32), tail], axis=1))
        update(sd, md_ref, ld_ref, accd_ref, lambda pr: _dot(pr, vp), LOG2E)
        rope = [_dot(qr, rbuf[slot, j].astype(BF16)) for j in range(cpp)]
        sm = _dot_nt(ql, cp_) + (rope[0] if cpp == 1 else jnp.concatenate(rope, axis=1))
        update(sm, mm_ref, lm_ref, accm_ref, lambda pr: _dot(pr, cp_), MLA_SCALE * LOG2E)

    nk = nk_ref[...].astype(BF16).astype(F32)
    nv = nv_ref[...].astype(BF16).astype(F32)
    nc = nc_ref[...].astype(BF16).astype(F32)
    nr = nr_ref[...].astype(BF16).astype(F32)
    s1 = jnp.sum(q2.astype(F32) * nk, axis=-1, keepdims=True) + dec_ref[:, PAGE:PAGE + 1]
    update(s1, md_ref, ld_ref, accd_ref, lambda pr: pr.astype(F32) * nv, LOG2E)
    s2 = (jnp.sum(ql.astype(F32) * nc, axis=-1, keepdims=True)
          + jnp.sum(qr.astype(F32) * nr, axis=-1, keepdims=True))
    update(s2, mm_ref, lm_ref, accm_ref, lambda pr: pr.astype(F32) * nc, MLA_SCALE * LOG2E)
    lam = _lambda(lam_ref, sc_ref[0])
    od = accd_ref[...] / ld_ref[...]
    o = od[0:N_HEADS] - lam * od[N_HEADS:2 * N_HEADS]
    oa_ref[...] = _rms(o, subln_ref[...]) * sc_ref[1]
    olat_ref[...] = accm_ref[...] / lm_ref[...]


def _chunk_pages(npages):
    return max(d for d in range(1, MAX_CHUNK_PAGES + 1) if npages % d == 0)


def _attn_sample(pt, ly, sc, lamv, dq, qlat, qrope, ck, cv, cc, cr_t, nk, nv, nc, nr, dec, subln):
    nseq, npages = pt.shape
    cpp = _chunk_pages(npages)
    nchunk = npages // cpp
    n = cpp * PAGE
    seq3 = lambda a: pl.BlockSpec((None,) + a.shape[1:], lambda b, pt, ly: (b, 0, 0))
    full = lambda a: pl.BlockSpec(a.shape, lambda b, pt, ly: (0,) * a.ndim)
    hbm = pl.BlockSpec(memory_space=pl.ANY)
    gs = pltpu.PrefetchScalarGridSpec(
        num_scalar_prefetch=2,
        grid=(nseq,),
        in_specs=[pl.BlockSpec(memory_space=pltpu.SMEM), full(lamv),
                  seq3(dq), seq3(qlat), seq3(qrope),
                  hbm, hbm, hbm, hbm,
                  seq3(nk), seq3(nv), seq3(nc), seq3(nr),
                  full(dec), full(subln)],
        out_specs=[pl.BlockSpec((None, N_HEADS, DIFF_V), lambda b, pt, ly: (b, 0, 0)),
                   pl.BlockSpec((None, N_HEADS, KV_RANK), lambda b, pt, ly: (b, 0, 0))],
        scratch_shapes=[
            pltpu.VMEM((2, n, DIFF_QK), F32), pltpu.VMEM((2, n, DIFF_V), F32),
            pltpu.VMEM((2, n, KV_RANK), F32), pltpu.VMEM((2, cpp, ROPE_DIM, PAGE), F32),
            pltpu.SemaphoreType.DMA((4, 2)),
            pltpu.VMEM((2 * N_HEADS, 1), F32), pltpu.VMEM((2 * N_HEADS, 1), F32),
            pltpu.VMEM((2 * N_HEADS, DIFF_V), F32),
            pltpu.VMEM((N_HEADS, 1), F32), pltpu.VMEM((N_HEADS, 1), F32),
            pltpu.VMEM((N_HEADS, KV_RANK), F32),
        ],
    )
    return pl.pallas_call(
        functools.partial(_attn_sample_kernel, cpp, nchunk),
        grid_spec=gs,
        out_shape=[jax.ShapeDtypeStruct((nseq, N_HEADS, DIFF_V), F32),
                   jax.ShapeDtypeStruct((nseq, N_HEADS, KV_RANK), F32)],
        compiler_params=_cparams(("arbitrary",)),
        name="attn_sample",
    )(pt.reshape(-1), ly, sc, lamv, dq, qlat, qrope, ck, cv, cc, cr_t, nk, nv, nc, nr, dec, subln)


def _uv_kernel(olat_ref, wuv_ref, ob_ref):
    for hd in range(N_HEADS):
        ob_ref[:, hd * LANES:(hd + 1) * LANES] = _dot(
            olat_ref[:, hd * KV_RANK:(hd + 1) * KV_RANK].astype(BF16), wuv_ref[hd]).astype(BF16)


def _uv_proj(olat, wuv):
    m = olat.shape[0]
    return pl.pallas_call(
        _uv_kernel,
        out_shape=jax.ShapeDtypeStruct((m, N_HEADS * MLA_V), BF16),
        compiler_params=pltpu.CompilerParams(vmem_limit_bytes=VMEM_LIMIT),
        name="uv_proj",
    )(olat, wuv)


MERGE_TN = 512


def _merge_kernel(h_ref, g_ref, oa_ref, ob_ref, wga_ref, wgb_ref, wba_ref, wbb_ref, wo_ref, o_ref,
                  u_ref, acc_ref):
    j = pl.program_id(1)

    @pl.when(j == 0)
    def _():
        u_ref[...] = _rms(h_ref[...], g_ref[...]).astype(BF16)
        acc_ref[...] = jnp.zeros_like(acc_ref)

    u = u_ref[...]
    ga = _dot(u, wga_ref[...])
    gb = _dot(u, wgb_ref[...])
    a = _dot(oa_ref[...], wba_ref[...])
    b = _dot(ob_ref[...], wbb_ref[...])
    mg = (jax.nn.sigmoid(ga) * a + jax.nn.sigmoid(gb) * b).astype(BF16)
    acc_ref[...] += _dot(mg, wo_ref[...])

    @pl.when(j == pl.num_programs(1) - 1)
    def _():
        o_ref[...] = h_ref[...] + acc_ref[...]


def _merge(h, g, oa, ob, wga, wgb, wba, wbb, wo, tm):
    m = h.shape[0]
    tn = MERGE_TN
    return pl.pallas_call(
        _merge_kernel,
        grid=(m // tm, D_MODEL // tn),
        in_specs=[
            pl.BlockSpec((tm, D_MODEL), lambda i, j: (i, 0)),
            pl.BlockSpec((1, D_MODEL), lambda i, j: (0, 0)),
            pl.BlockSpec((tm, 1024), lambda i, j: (i, 0)),
            pl.BlockSpec((tm, 1024), lambda i, j: (i, 0)),
            pl.BlockSpec((D_MODEL, tn), lambda i, j: (0, j)),
            pl.BlockSpec((D_MODEL, tn), lambda i, j: (0, j)),
            pl.BlockSpec((1024, tn), lambda i, j: (0, j)),
            pl.BlockSpec((1024, tn), lambda i, j: (0, j)),
            pl.BlockSpec((tn, D_MODEL), lambda i, j: (j, 0)),
        ],
        out_specs=pl.BlockSpec((tm, D_MODEL), lambda i, j: (i, 0)),
        out_shape=jax.ShapeDtypeStruct((m, D_MODEL), F32),
        scratch_shapes=[pltpu.VMEM((tm, D_MODEL), BF16), pltpu.VMEM((tm, D_MODEL), F32)],
        compiler_params=_cparams(("parallel", "arbitrary")),
        name="merge",
    )(h, g, oa, ob, wga, wgb, wba, wbb, wo)


PLE_TN = 512


def _ple_kernel(h_ref, hc_ref, g_ref, p_ref, wpg_ref, wpp_ref, o_ref, hn_ref):
    @pl.when(pl.program_id(1) == 0)
    def _():
        hn_ref[...] = _rms(h_ref[...], g_ref[...]).astype(BF16)

    gate = jax.nn.sigmoid(_dot(hn_ref[...], wpg_ref[...]))
    proj = _dot(p_ref[...].astype(BF16), wpp_ref[...])
    o_ref[...] = hc_ref[...] + gate * proj


def _ple(h, g, p, wpg, wpp, tm):
    m = h.shape[0]
    tn = PLE_TN
    return pl.pallas_call(
        _ple_kernel,
        grid=(m // tm, D_MODEL // tn),
        in_specs=[
            pl.BlockSpec((tm, D_MODEL), lambda i, j: (i, 0)),
            pl.BlockSpec((tm, tn), lambda i, j: (i, j)),
            pl.BlockSpec((1, D_MODEL), lambda i, j: (0, 0)),
            pl.BlockSpec((tm, PLE_DIM), lambda i, j: (i, 0)),
            pl.BlockSpec((D_MODEL, tn), lambda i, j: (0, j)),
            pl.BlockSpec((PLE_DIM, tn), lambda i, j: (0, j)),
        ],
        out_specs=pl.BlockSpec((tm, tn), lambda i, j: (i, j)),
        out_shape=jax.ShapeDtypeStruct((m, D_MODEL), F32),
        scratch_shapes=[pltpu.VMEM((tm, D_MODEL), BF16)],
        compiler_params=_cparams(("parallel", "arbitrary")),
        name="ple",
    )(h, h, g, p, wpg, wpp)


def _norm_kernel(x_ref, g_ref, o_ref):
    o_ref[...] = _rms(x_ref[...], g_ref[...])


def _final_norm(x, g, tm):
    m = x.shape[0]
    return pl.pallas_call(
        _norm_kernel,
        grid=(m // tm,),
        in_specs=[pl.BlockSpec((tm, D_MODEL), lambda i: (i, 0)), pl.BlockSpec((1, D_MODEL), lambda i: (0, 0))],
        out_specs=pl.BlockSpec((tm, D_MODEL), lambda i: (i, 0)),
        out_shape=jax.ShapeDtypeStruct((m, D_MODEL), F32),
        compiler_params=_cparams(("parallel",)),
        name="final_norm",
    )(x, g)


def _swap_halves(w):
    half = ROPE_DIM // 2
    return jnp.concatenate([w[:, half:], w[:, :half]], axis=1)


def _prep_layer(w_ffn_in, w_ffn_out, w_in, w_uk, w_uv):
    pad_c = lambda w: jnp.pad(w, ((0, 0), (0, D_FF_PAD - D_FF))).astype(BF16)
    wg = pad_c(w_ffn_in[:, :D_FF])
    wu = pad_c(w_ffn_in[:, D_FF:])
    wo = jnp.pad(w_ffn_out, ((0, D_FF_PAD - D_FF), (0, 0))).astype(BF16)
    out = {"ffn": (wg, wu, wo)}
    if w_in is not None:
        per = MLA_NOPE + ROPE_DIM
        nope = [w_in[:, _OFF_MQ + h * per:_OFF_MQ + h * per + MLA_NOPE] for h in range(N_HEADS)]
        rope = [w_in[:, _OFF_MQ + h * per + MLA_NOPE:_OFF_MQ + (h + 1) * per] for h in range(N_HEADS)]
        kr = w_in[:, _OFF_KR:_OFF_KR + ROPE_DIM]
        wa = jnp.concatenate([w_in[:, _OFF_DQ:_OFF_DQ + 1024]] + nope + rope + [_swap_halves(r) for r in rope],
                             axis=1).astype(BF16)
        wb = jnp.concatenate([w_in[:, _OFF_DK:_OFF_DK + 256], w_in[:, _OFF_CKV:_OFF_CKV + KV_RANK],
                              kr, _swap_halves(kr)], axis=1).astype(BF16)
        out["wa"] = wa
        out["wb"] = wb
        out["wga"] = w_in[:, _OFF_GA:_OFF_GA + D_MODEL].astype(BF16)
        out["wgb"] = w_in[:, _OFF_GB:_OFF_GB + D_MODEL].astype(BF16)
        out["wuk"] = jnp.transpose(w_uk, (1, 2, 0)).astype(BF16)
        out["wuv"] = jnp.transpose(w_uv, (1, 0, 2)).astype(BF16)
        out["wuvt"] = jnp.transpose(w_uv, (1, 2, 0)).astype(BF16)
    return out


def _rope_tables(pos):
    half = ROPE_DIM // 2
    inv = ROPE_THETA ** (-jnp.arange(half, dtype=F32) / half)
    ang = pos.astype(F32)[:, None] * inv[None, :]
    cos = jnp.cos(ang)
    sin = jnp.sin(ang)
    cos128 = jnp.concatenate([cos, cos, cos, cos], axis=1)
    sinm128 = jnp.concatenate([-sin, sin, -sin, sin], axis=1)
    cs128 = jnp.concatenate([cos, cos, -sin, sin], axis=1)
    return cos128, sinm128, cs128


def _tile_m(m):
    return min(m, 512)


def kernel(x_prompt, x_sample, cache_diff_k, cache_diff_v, cache_mla_ckv, cache_mla_krope, page_table,
           p_prompt, p_sample, rel_bias, norm_ffn1, w_ffn1_in, w_ffn1_out, norm_mix, w_in,
           diff_lambda_q1, diff_lambda_k1, diff_lambda_q2, diff_lambda_k2, diff_subln, mla_kv_norm,
           mla_w_uk, mla_w_uv, w_branch_a, w_branch_b, w_out, norm_ffn2, w_ffn2_in, w_ffn2_out,
           norm_ple, w_ple_gate, w_ple_proj, norm_final):
    nb, s, _ = x_prompt.shape
    nseq = x_sample.shape[0]
    assert x_sample.shape[1] == 1 and cache_diff_k.shape[2] == PAGE and s % ATTN_T == 0
    past_len = page_table.shape[1] * PAGE
    mp = nb * s
    hp = x_prompt.reshape(mp, D_MODEL)
    hs = x_sample.reshape(nseq, D_MODEL)
    tmp, tms = _tile_m(mp), _tile_m(nseq)

    rope_p = _rope_tables(jnp.tile(jnp.arange(s, dtype=jnp.int32), nb))
    rope_s = _rope_tables(jnp.full((nseq,), past_len, jnp.int32))
    delta, dec = _bias_tiles(rel_bias, ATTN_T)
    cache_kr_t = jnp.swapaxes(cache_mla_krope, 2, 3)
    row = lambda v: v.reshape(1, -1)

    rows_p = ([], [], [], [])
    rows_s = ([], [], [], [])
    for i in range(DEPTH):
        w1 = _prep_layer(w_ffn1_in[i], w_ffn1_out[i], w_in[i], mla_w_uk[i], mla_w_uv[i])
        w2 = _prep_layer(w_ffn2_in[i], w_ffn2_out[i], None, None, None)
        wba = w_branch_a[i].astype(BF16)
        wbb = w_branch_b[i].astype(BF16)
        wout = w_out[i].astype(BF16)
        wpg = w_ple_gate[i].astype(BF16)
        wpp = w_ple_proj[i].astype(BF16)
        lam_init = 0.8 - 0.6 * math.exp(-0.3 * i)
        sc = jnp.array([lam_init, 1.0 - lam_init], F32)
        lamv = jnp.stack([diff_lambda_q1[i], diff_lambda_k1[i], diff_lambda_q2[i], diff_lambda_k2[i]])
        subln = row(diff_subln[i])
        kvn = row(mla_kv_norm[i])
        ly = jnp.array([i], jnp.int32)

        hp = _ffn(hp, row(norm_ffn1[i]), *w1["ffn"], tmp)
        (dq, qlat, qrope, dk, dv, ckv, kr, kb, vt, kcat, ct) = _qkv(
            True, hp, row(norm_mix[i]), w1["wa"], w1["wb"], w1["wuk"], kvn, *rope_p, ATTN_T)
        oa, ob = _attn_prompt(sc, lamv, dq, qlat, qrope, kb, vt, kcat, ct, delta,
                              diff_subln[i].reshape(-1, 1), w1["wuvt"], nb, s)
        hp = _merge(hp, row(norm_mix[i]), oa, ob, w1["wga"], w1["wgb"], wba, wbb, wout, tmp)
        hp = _ffn(hp, row(norm_ffn2[i]), *w2["ffn"], tmp)
        hp = _ple(hp, row(norm_ple[i]), p_prompt[i].reshape(mp, PLE_DIM), wpg, wpp, tmp)
        for lst, r in zip(rows_p, (dk, dv, ckv, kr)):
            lst.append(r.reshape(nb, s, -1))

        hs = _ffn(hs, row(norm_ffn1[i]), *w1["ffn"], tms)
        (dq, qlat, qrope, dk, dv, ckv, kr) = _qkv(
            False, hs, row(norm_mix[i]), w1["wa"], w1["wb"], w1["wuk"], kvn, *rope_s, min(nseq, ATTN_T))
        oa3, olat3 = _attn_sample(
            page_table, ly, sc, lamv,
            dq.reshape(nseq, N_HEADS, DIFF_QK), qlat.reshape(nseq, N_HEADS, KV_RANK),
            qrope.reshape(nseq, N_HEADS, ROPE_DIM),
            cache_diff_k, cache_diff_v, cache_mla_ckv, cache_kr_t,
            dk.reshape(nseq, 1, -1), dv.reshape(nseq, 1, -1), ckv.reshape(nseq, 1, -1), kr.reshape(nseq, 1, -1),
            dec, subln)
        ob = _uv_proj(olat3.reshape(nseq, N_HEADS * KV_RANK), w1["wuv"])
        hs = _merge(hs, row(norm_mix[i]), oa3.reshape(nseq, N_HEADS * DIFF_V).astype(BF16), ob,
                    w1["wga"], w1["wgb"], wba, wbb, wout, tms)
        hs = _ffn(hs, row(norm_ffn2[i]), *w2["ffn"], tms)
        hs = _ple(hs, row(norm_ple[i]), p_sample[i].reshape(nseq, PLE_DIM), wpg, wpp, tms)
        for lst, r in zip(rows_s, (dk, dv, ckv, kr)):
            lst.append(r.reshape(nseq, 1, -1))

    y_prompt = _final_norm(hp, row(norm_final), tmp).reshape(nb, s, D_MODEL)
    y_sample = _final_norm(hs, row(norm_final), tms).reshape(nseq, 1, D_MODEL)
    return (y_prompt, y_sample,
            jnp.stack(rows_p[0]), jnp.stack(rows_p[1]), jnp.stack(rows_p[2]), jnp.stack(rows_p[3]),
            jnp.stack(rows_s[0]), jnp.stack(rows_s[1]), jnp.stack(rows_s[2]), jnp.stack(rows_s[3]))
```
